```python
import math
import jax, jax.numpy as jnp
from jax import lax
import numpy as np

D_MODEL = 1024
BATCH = 8
SEQ = 2048
DEPTH = 2
DEC_BATCH = 128
DEC_SEQ = 1
PAST_LEN = 16384
PAGE_SIZE = 128

W_A = D_MODEL
N_BLOCKS_A = 8
BLK_A = W_A // N_BLOCKS_A
CONV_A = 4
LRU_C = 8.0
W_B = D_MODEL // 2
CONV_B = 31
W_C = D_MODEL // 2
GS_C = 16
G_C = W_C // GS_C
P_C = 64
N_BRANCH = 3
IN_W = W_A + 2 * W_B + W_C + N_BRANCH * D_MODEL
D_FF = int(math.ceil(8 * D_MODEL / 3 / 256) * 256)
EPS = 1e-6

kernel_name = 'hybrid_rglru_conformer_s5_step'


def rms_norm(x, g):
    xf = x.astype(jnp.float32)
    y = xf * lax.rsqrt(jnp.mean(xf * xf, axis=-1, keepdims=True) + EPS)
    return (y * g.astype(jnp.float32)).astype(x.dtype)


def layer_norm(x, g, b):
    xf = x.astype(jnp.float32)
    mu = jnp.mean(xf, axis=-1, keepdims=True)
    xc = xf - mu
    y = xc * lax.rsqrt(jnp.mean(xc * xc, axis=-1, keepdims=True) + EPS)
    return (y * g.astype(jnp.float32) + b.astype(jnp.float32)).astype(x.dtype)


def causal_dwconv(u, buf, w, b):
    k = w.shape[0]
    full = jnp.concatenate([buf.astype(u.dtype), u], axis=1)
    y = lax.conv_general_dilated(full, w[:, None, :].astype(u.dtype), window_strides=(1,), padding='VALID',
                                 dimension_numbers=('NWC', 'WIO', 'NWC'), feature_group_count=u.shape[-1])
    return y + b.astype(u.dtype), full[:, -(k - 1):]


def _lin_combine(e1, e2):
    a1, b1 = e1
    a2, b2 = e2
    return a1 * a2, a2 * b1 + b2


def _cplx_combine(e1, e2):
    a1r, a1i, b1r, b1i = e1
    a2r, a2i, b2r, b2i = e2
    return (a2r * a1r - a2i * a1i, a2r * a1i + a2i * a1r,
            a2r * b1r - a2i * b1i + b2r, a2r * b1i + a2i * b1r + b2i)


def rg_lru(u, h0, w_rg, b_rg, w_ig, b_ig, lam_a):
    n, t, w = u.shape
    ub = u.reshape(n, t, N_BLOCKS_A, BLK_A)
    r = jax.nn.sigmoid(jnp.einsum('nthi,hij->nthj', ub, w_rg).reshape(n, t, w).astype(jnp.float32) + b_rg.astype(jnp.float32))
    ig = jax.nn.sigmoid(jnp.einsum('nthi,hij->nthj', ub, w_ig).reshape(n, t, w).astype(jnp.float32) + b_ig.astype(jnp.float32))
    log_a = -LRU_C * r * jax.nn.softplus(-lam_a.astype(jnp.float32))
    a = jnp.exp(log_a)
    bx = jnp.sqrt(-jnp.expm1(2.0 * log_a)) * ig * u.astype(jnp.float32)
    bx = bx.at[:, 0].add(a[:, 0] * h0.astype(jnp.float32))
    _, h = lax.associative_scan(_lin_combine, (a, bx), axis=1)
    return h.astype(u.dtype), h[:, -1]


def s5_ssm(u, s0_re, s0_im, lam_re, lam_im, log_dt, b_re, b_im, c_re, c_im, d_skip):
    n, t, w = u.shape
    uf = u.astype(jnp.float32)
    ug = uf.reshape(n, t, G_C, GS_C)
    dt = jnp.exp(log_dt.astype(jnp.float32))[:, None]
    lr, li = lam_re.astype(jnp.float32), lam_im.astype(jnp.float32)
    mag = jnp.exp(lr * dt)
    ar, ai = mag * jnp.cos(li * dt), mag * jnp.sin(li * dt)
    den = lr * lr + li * li
    qr = ((ar - 1.0) * lr + ai * li) / den
    qi = (ai * lr - (ar - 1.0) * li) / den
    br, bi = b_re.astype(jnp.float32), b_im.astype(jnp.float32)
    bbr = qr[..., None] * br - qi[..., None] * bi
    bbi = qr[..., None] * bi + qi[..., None] * br
    xr = jnp.einsum('ntgc,gpc->ntgp', ug, bbr)
    xi = jnp.einsum('ntgc,gpc->ntgp', ug, bbi)
    s0r, s0i = s0_re.astype(jnp.float32), s0_im.astype(jnp.float32)
    xr = xr.at[:, 0].add(ar * s0r - ai * s0i)
    xi = xi.at[:, 0].add(ar * s0i + ai * s0r)
    arb = jnp.broadcast_to(ar, xr.shape)
    aib = jnp.broadcast_to(ai, xi.shape)
    _, _, sr, si = lax.associative_scan(_cplx_combine, (arb, aib, xr, xi), axis=1)
    y = (jnp.einsum('ntgp,gcp->ntgc', sr, c_re.astype(jnp.float32))
         - jnp.einsum('ntgp,gcp->ntgc', si, c_im.astype(jnp.float32))).reshape(n, t, w)
    y = y + d_skip.astype(jnp.float32) * uf
    return y.astype(u.dtype), sr[:, -1], si[:, -1]


def layer(x, conv_a, h_a, conv_b, s_re, s_im,
          g_mix, w_in, w_conv_a, b_conv_a, w_rg, b_rg, w_ig, b_ig, lam_a,
          w_dw_b, b_dw_b, ln_g_b, ln_b_b,
          lam_re, lam_im, log_dt, b_ssm_re, b_ssm_im, c_ssm_re, c_ssm_im, d_ssm, w_glu_c, b_glu_c,
          b_gate, w_pa, w_pb, w_pc, w_out, g_ffn, w_ffn_in, w_ffn_out):
    h = rms_norm(x, g_mix)
    proj = jnp.einsum('ntd,de->nte', h, w_in)
    o1, o2, o3 = W_A, W_A + 2 * W_B, W_A + 2 * W_B + W_C
    u_a, z_b, u_c, gate_in = proj[..., :o1], proj[..., o1:o2], proj[..., o2:o3], proj[..., o3:]
    c_a, new_conv_a = causal_dwconv(u_a, conv_a, w_conv_a, b_conv_a)
    y_a, new_h = rg_lru(c_a, h_a, w_rg, b_rg, w_ig, b_ig, lam_a)
    glu_b = z_b[..., :W_B] * jax.nn.sigmoid(z_b[..., W_B:])
    c_b, new_conv_b = causal_dwconv(glu_b, conv_b, w_dw_b, b_dw_b)
    y_b = jax.nn.silu(layer_norm(c_b, ln_g_b, ln_b_b))
    y_c, new_s_re, new_s_im = s5_ssm(u_c, s_re, s_im, lam_re, lam_im, log_dt, b_ssm_re, b_ssm_im, c_ssm_re, c_ssm_im, d_ssm)
    y_c = jax.nn.gelu(y_c)
    y_c = y_c * jax.nn.sigmoid(jnp.einsum('ntc,ce->nte', y_c, w_glu_c) + b_glu_c)
    gates = jax.nn.sigmoid(gate_in + b_gate)
    g_a, g_b, g_c = gates[..., :D_MODEL], gates[..., D_MODEL:2 * D_MODEL], gates[..., 2 * D_MODEL:]
    merged = (g_a * jnp.einsum('ntc,cd->ntd', y_a, w_pa)
              + g_b * jnp.einsum('ntc,cd->ntd', y_b, w_pb)
              + g_c * jnp.einsum('ntc,cd->ntd', y_c, w_pc))
    x = x + jnp.einsum('ntd,de->nte', merged, w_out)
    h2 = rms_norm(x, g_ffn)
    gu = jnp.einsum('ntd,df->ntf', h2, w_ffn_in)
    x = x + jnp.einsum('ntf,fd->ntd', jax.nn.silu(gu[..., :D_FF]) * gu[..., D_FF:], w_ffn_out)
    return x, (new_conv_a, new_h.astype(x.dtype), new_conv_b, new_s_re.astype(x.dtype), new_s_im.astype(x.dtype))


def trunk(x, states, layer_params, g_final):
    outs = ([], [], [], [], [])
    for l in range(DEPTH):
        st = [s[l] for s in states]
        x, new = layer(x, *st, *[p[l] for p in layer_params])
        for o, v in zip(outs, new):
            o.append(v)
    stacked = [jnp.stack(o, axis=0) for o in outs]
    return rms_norm(x, g_final), stacked


def setup_inputs(seed: int = 0) -> dict:
    key = jax.random.key(seed)
    ks = iter(jax.random.split(key, 64))
    f32 = jnp.float32
    L = DEPTH

    def nrm(shape, s):
        return jax.random.normal(next(ks), shape, f32) * s

    u0 = jax.random.uniform(next(ks), (L, W_A), f32, minval=0.9, maxval=0.999)
    a_base = u0 ** (1.0 / LRU_C)
    lam_a = jnp.log(a_base) - jnp.log1p(-a_base)
    lam_im = jnp.broadcast_to(jnp.pi * jnp.arange(P_C, dtype=f32), (L, G_C, P_C)) + nrm((L, G_C, P_C), 0.01)
    log_dt = jax.random.uniform(next(ks), (L, G_C), f32, minval=math.log(0.001), maxval=math.log(0.1))
    return {
        'x_prompt': nrm((BATCH, SEQ, D_MODEL), 1.0),
        'x_sample': nrm((DEC_BATCH, DEC_SEQ, D_MODEL), 1.0),
        'state_lru_conv': nrm((L, DEC_BATCH, CONV_A - 1, W_A), 1.0),
        'state_lru_h': nrm((L, DEC_BATCH, W_A), 0.5),
        'state_cfm_conv': nrm((L, DEC_BATCH, CONV_B - 1, W_B), 1.0),
        'state_ssm_re': nrm((L, DEC_BATCH, G_C, P_C), 0.1),
        'state_ssm_im': nrm((L, DEC_BATCH, G_C, P_C), 0.1),
        'g_mix': 1.0 + nrm((L, D_MODEL), 0.02),
        'w_in': nrm((L, D_MODEL, IN_W), D_MODEL ** -0.5),
        'w_conv_a': nrm((L, CONV_A, W_A), CONV_A ** -0.5),
        'b_conv_a': nrm((L, W_A), 0.01),
        'w_rg': nrm((L, N_BLOCKS_A, BLK_A, BLK_A), BLK_A ** -0.5),
        'b_rg': nrm((L, W_A), 0.01),
        'w_ig': nrm((L, N_BLOCKS_A, BLK_A, BLK_A), BLK_A ** -0.5),
        'b_ig': nrm((L, W_A), 0.01),
        'lam_a': lam_a,
        'w_dw_b': nrm((L, CONV_B, W_B), CONV_B ** -0.5),
        'b_dw_b': nrm((L, W_B), 0.01),
        'ln_g_b': 1.0 + nrm((L, W_B), 0.02),
        'ln_b_b': nrm((L, W_B), 0.01),
        'lam_re': -0.5 + nrm((L, G_C, P_C), 0.01),
        'lam_im': lam_im,
        'log_dt': log_dt,
        'b_ssm_re': nrm((L, G_C, P_C, GS_C), (2.0 * GS_C) ** -0.5),
        'b_ssm_im': nrm((L, G_C, P_C, GS_C), (2.0 * GS_C) ** -0.5),
        'c_ssm_re': nrm((L, G_C, GS_C, P_C), P_C ** -0.5),
        'c_ssm_im': nrm((L, G_C, GS_C, P_C), P_C ** -0.5),
        'd_ssm': nrm((L, W_C), 1.0),
        'w_glu_c': nrm((L, W_C, W_C), W_C ** -0.5),
        'b_glu_c': nrm((L, W_C), 0.01),
        'b_gate': nrm((L, N_BRANCH * D_MODEL), 0.01),
        'w_pa': nrm((L, W_A, D_MODEL), W_A ** -0.5),
        'w_pb': nrm((L, W_B, D_MODEL), W_B ** -0.5),
        'w_pc': nrm((L, W_C, D_MODEL), W_C ** -0.5),
        'w_out': nrm((L, D_MODEL, D_MODEL), D_MODEL ** -0.5),
        'g_ffn': 1.0 + nrm((L, D_MODEL), 0.02),
        'w_ffn_in': nrm((L, D_MODEL, 2 * D_FF), D_MODEL ** -0.5),
        'w_ffn_out': nrm((L, D_FF, D_MODEL), D_FF ** -0.5),
        'g_final': 1.0 + nrm((D_MODEL,), 0.02),
    }


def reference(x_prompt, x_sample, state_lru_conv, state_lru_h, state_cfm_conv, state_ssm_re, state_ssm_im,
              g_mix, w_in, w_conv_a, b_conv_a, w_rg, b_rg, w_ig, b_ig, lam_a,
              w_dw_b, b_dw_b, ln_g_b, ln_b_b,
              lam_re, lam_im, log_dt, b_ssm_re, b_ssm_im, c_ssm_re, c_ssm_im, d_ssm, w_glu_c, b_glu_c,
              b_gate, w_pa, w_pb, w_pc, w_out, g_ffn, w_ffn_in, w_ffn_out, g_final):
    layer_params = (g_mix, w_in, w_conv_a, b_conv_a, w_rg, b_rg, w_ig, b_ig, lam_a,
                    w_dw_b, b_dw_b, ln_g_b, ln_b_b,
                    lam_re, lam_im, log_dt, b_ssm_re, b_ssm_im, c_ssm_re, c_ssm_im, d_ssm, w_glu_c, b_glu_c,
                    b_gate, w_pa, w_pb, w_pc, w_out, g_ffn, w_ffn_in, w_ffn_out)
    dt = x_prompt.dtype
    nb = x_prompt.shape[0]
    prompt_states = (jnp.zeros((DEPTH, nb, CONV_A - 1, W_A), dt),
                     jnp.zeros((DEPTH, nb, W_A), dt),
                     jnp.zeros((DEPTH, nb, CONV_B - 1, W_B), dt),
                     jnp.zeros((DEPTH, nb, G_C, P_C), dt),
                     jnp.zeros((DEPTH, nb, G_C, P_C), dt))
    y_prompt, p_new = trunk(x_prompt, prompt_states, layer_params, g_final)
    sample_states = (state_lru_conv, state_lru_h, state_cfm_conv, state_ssm_re, state_ssm_im)
    y_sample, s_new = trunk(x_sample, sample_states, layer_params, g_final)
    p_lru_conv, p_lru_h, p_cfm_conv, p_ssm_re, p_ssm_im = p_new
    s_lru_conv, s_lru_h, s_cfm_conv, s_ssm_re, s_ssm_im = s_new
    return (y_prompt, y_sample, p_lru_conv, p_lru_h, p_cfm_conv, p_ssm_re, p_ssm_im,
            s_lru_conv, s_lru_h, s_cfm_conv, s_ssm_re, s_ssm_im)
```

```python
import functools
import math

import jax
import jax.numpy as jnp
from jax import lax
from jax.experimental import pallas as pl
from jax.experimental.pallas import tpu as pltpu

D_MODEL = 1024
DEPTH = 2
W_A = D_MODEL
N_BLOCKS_A = 8
BLK_A = W_A // N_BLOCKS_A
CONV_A = 4
LRU_C = 8.0
W_B = D_MODEL // 2
CONV_B = 31
W_C = D_MODEL // 2
GS_C = 16
G_C = W_C // GS_C
P_C = 64
S_C = G_C * P_C
N_BRANCH = 3
IN_W = W_A + 2 * W_B + W_C + N_BRANCH * D_MODEL
D_FF = int(math.ceil(8 * D_MODEL / 3 / 256) * 256)
EPS = 1e-6

V7X_SUBLANES = 8
V7X_LANES = 128
V7X_MXU_DIM = 256
V7X_VMEM_BYTES = 64 * 1024 * 1024

OFF_ZB = W_A
OFF_UC = W_A + 2 * W_B
OFF_GATE = W_A + 2 * W_B + W_C
S5_LANE_GROUP = 4 * V7X_LANES
S5_CHUNKS = W_C // V7X_LANES
S5_CHUNK_STATE = S_C // S5_CHUNKS

BF16 = jnp.bfloat16
F32 = jnp.float32


def _sigmoid(x):
    return 1.0 / (1.0 + jnp.exp(-x))


def _silu(x):
    return x * _sigmoid(x)


def _gelu_tanh(x):
    c = math.sqrt(2.0 / math.pi)
    return 0.5 * x * (1.0 + jnp.tanh(c * (x + 0.044715 * (x * x * x))))


def _softplus(x):
    return jnp.maximum(x, 0.0) + jnp.log1p(jnp.exp(-jnp.abs(x)))


def _rms_norm(x, g):
    return x * lax.rsqrt(jnp.mean(x * x, axis=-1, keepdims=True) + EPS) * g


def _dot(a, b):
    return jnp.dot(a, b, preferred_element_type=F32)


def _dwconv(tap, w_ref, bias_ref, out_ref, taps, rows, rb):
    qb = rb // V7X_SUBLANES
    c = out_ref.shape[-1]

    def body(i, carry):
        q0 = pl.multiple_of(i * qb, qb)
        acc = jnp.broadcast_to(bias_ref[...][None], (qb, V7X_SUBLANES, c))
        for k in range(taps):
            acc = acc + w_ref[k][None] * tap(k, q0, qb)
        out_ref[pl.ds(q0, qb)] = acc
        return carry

    lax.fori_loop(0, rows // rb, body, 0)


def _taps(hist_ref, buf_ref, taps, qnb, single_step):
    if single_step:
        def tap(k, q0, qb):
            if k < taps - 1:
                return hist_ref[pl.ds(k * qnb + q0, qb)]
            return buf_ref[pl.ds(q0, qb)]
    else:
        def tap(k, q0, qb):
            return buf_ref[pl.ds(q0 + k * qnb, qb)]
    return tap


def _mixer_kernel(
        x_ref, st_ca_ref, st_h_ref, st_cb_ref, st_sr_ref, st_si_ref,
        g_mix_ref, w_in_ref, wca_ref, bca_ref, wrgig_ref, b_rg_ref, b_ig_ref, lam_a_ref,
        wdb_ref, bdb_ref, ln_g_ref, ln_b_ref,
        ar_ref, ai_ref, bmat_ref, cre_ref, cim_ref, d_ssm_ref, w_glu_ref, b_glu_ref,
        b_gate_ref, w_pa_ref, w_pb_ref, w_pc_ref, w_out_ref,
        xo_ref, o_ca_ref, o_h_ref, o_cb_ref, o_sr_ref, o_si_ref,
        ua_buf, ca_buf, gb_buf, cb_buf, a_buf, bx_buf, xr_buf, xi_buf, h_st, sr_st, si_st,
        *, nb, tt, n_steps):
    rows = nb * tt
    qnb = nb // V7X_SUBLANES
    qrows = rows // V7X_SUBLANES
    step = pl.program_id(0)
    single_step = tt == 1
    hist_a = 0 if single_step else (CONV_A - 1) * qnb
    hist_b = 0 if single_step else (CONV_B - 1) * qnb

    @pl.when(step == 0)
    def _():
        if not single_step:
            ua_buf[pl.ds(0, hist_a)] = st_ca_ref[...]
            gb_buf[pl.ds(0, hist_b)] = st_cb_ref[...]
        h_st[...] = st_h_ref[...]
        sr_st[...] = st_sr_ref[...]
        si_st[...] = st_si_ref[...]

    x = x_ref[...]
    h = _rms_norm(x, g_mix_ref[...]).astype(BF16)

    u_a = _dot(h, w_in_ref[:, 0:W_A])
    ua_buf[pl.ds(hist_a, qrows)] = u_a.reshape(qrows, V7X_SUBLANES, W_A)
    _dwconv(_taps(st_ca_ref, ua_buf, CONV_A, qnb, single_step), wca_ref, bca_ref, ca_buf,
            CONV_A, rows, min(rows, 16))
    neg_c_softplus = -LRU_C * _softplus(-lam_a_ref[...])
    for i in range(N_BLOCKS_A):
        sl = slice(i * BLK_A, (i + 1) * BLK_A)
        c_blk = ca_buf[:, :, sl].reshape(rows, BLK_A)
        pre = _dot(c_blk.astype(BF16), wrgig_ref[i])
        r = _sigmoid(pre[:, :BLK_A] + b_rg_ref[:, sl])
        ig = _sigmoid(pre[:, BLK_A:] + b_ig_ref[:, sl])
        log_a = r * neg_c_softplus[:, sl]
        a = jnp.exp(log_a)
        a_buf[:, sl] = a
        bx_buf[:, sl] = jnp.sqrt(-jnp.tanh(log_a) * (a * a + 1.0)) * ig * c_blk

    if tt == 1:
        h_last = a_buf[...] * h_st[...] + bx_buf[...]
        bx_buf[...] = h_last
    else:
        def lru_body(t, hc):
            r0 = pl.multiple_of(t * nb, nb)
            hn = a_buf[pl.ds(r0, nb), :] * hc + bx_buf[pl.ds(r0, nb), :]
            bx_buf[pl.ds(r0, nb), :] = hn
            return hn
        h_last = lax.fori_loop(0, tt, lru_body, h_st[...], unroll=4)
    h_st[...] = h_last
    pa = _dot(bx_buf[...].astype(BF16), w_pa_ref[...])
    merged = _sigmoid(_dot(h, w_in_ref[:, OFF_GATE:OFF_GATE + D_MODEL])
                      + b_gate_ref[:, 0:D_MODEL]) * pa

    z_b = _dot(h, w_in_ref[:, OFF_ZB:OFF_ZB + 2 * W_B])
    glu_b = z_b[:, :W_B] * _sigmoid(z_b[:, W_B:])
    gb_buf[pl.ds(hist_b, qrows)] = glu_b.reshape(qrows, V7X_SUBLANES, W_B)
    _dwconv(_taps(st_cb_ref, gb_buf, CONV_B, qnb, single_step), wdb_ref, bdb_ref, cb_buf,
            CONV_B, rows, min(rows, 32))
    c_b = cb_buf[...].reshape(rows, W_B)
    mu = jnp.mean(c_b, axis=-1, keepdims=True)
    xc = c_b - mu
    ln = xc * lax.rsqrt(jnp.mean(xc * xc, axis=-1, keepdims=True) + EPS)
    y_b = _silu(ln * ln_g_ref[...] + ln_b_ref[...])
    pb = _dot(y_b.astype(BF16), w_pb_ref[...])
    merged = merged + _sigmoid(_dot(h, w_in_ref[:, OFF_GATE + D_MODEL:OFF_GATE + 2 * D_MODEL])
                               + b_gate_ref[:, D_MODEL:2 * D_MODEL]) * pb

    u_c = _dot(h, w_in_ref[:, OFF_UC:OFF_UC + W_C])
    for j in range(S5_CHUNKS):
        xcat = _dot(u_c[:, j * V7X_LANES:(j + 1) * V7X_LANES].astype(BF16), bmat_ref[j])
        sl = slice(j * S5_CHUNK_STATE, (j + 1) * S5_CHUNK_STATE)
        xr_buf[:, sl] = xcat[:, :S5_CHUNK_STATE]
        xi_buf[:, sl] = xcat[:, S5_CHUNK_STATE:]
    for g4 in range(S_C // S5_LANE_GROUP):
        sl = slice(g4 * S5_LANE_GROUP, (g4 + 1) * S5_LANE_GROUP)
        ar_b = jnp.broadcast_to(ar_ref[:, sl], (nb, S5_LANE_GROUP))
        ai_b = jnp.broadcast_to(ai_ref[:, sl], (nb, S5_LANE_GROUP))
        if tt == 1:
            sr0, si0 = sr_st[:, sl], si_st[:, sl]
            nsr = ar_b * sr0 - ai_b * si0 + xr_buf[:, sl]
            nsi = ar_b * si0 + ai_b * sr0 + xi_buf[:, sl]
            xr_buf[:, sl] = nsr
            xi_buf[:, sl] = nsi
        else:
            def s5_body(t, carry, sl=sl, ar_b=ar_b, ai_b=ai_b):
                sr0, si0 = carry
                r0 = pl.multiple_of(t * nb, nb)
                nsr = ar_b * sr0 - ai_b * si0 + xr_buf[pl.ds(r0, nb), sl]
                nsi = ar_b * si0 + ai_b * sr0 + xi_buf[pl.ds(r0, nb), sl]
                xr_buf[pl.ds(r0, nb), sl] = nsr
                xi_buf[pl.ds(r0, nb), sl] = nsi
                return nsr, nsi
            nsr, nsi = lax.fori_loop(0, tt, s5_body, (sr_st[:, sl], si_st[:, sl]), unroll=4)
        sr_st[:, sl] = nsr
        si_st[:, sl] = nsi
    y_parts = []
    for j in range(S5_CHUNKS):
        sl = slice(j * S5_CHUNK_STATE, (j + 1) * S5_CHUNK_STATE)
        y_parts.append(_dot(xr_buf[:, sl].astype(BF16), cre_ref[j])
                       - _dot(xi_buf[:, sl].astype(BF16), cim_ref[j]))
    y_c = jnp.concatenate(y_parts, axis=-1) + d_ssm_ref[...] * u_c
    y_c = _gelu_tanh(y_c)
    y_c = y_c * _sigmoid(_dot(y_c.astype(BF16), w_glu_ref[...]) + b_glu_ref[...])
    pc = _dot(y_c.astype(BF16), w_pc_ref[...])
    merged = merged + _sigmoid(_dot(h, w_in_ref[:, OFF_GATE + 2 * D_MODEL:OFF_GATE + 3 * D_MODEL])
                               + b_gate_ref[:, 2 * D_MODEL:3 * D_MODEL]) * pc

    xo_ref[...] = x + _dot(merged.astype(BF16), w_out_ref[...])

    if single_step:
        o_ca_ref[pl.ds(0, (CONV_A - 2) * qnb)] = st_ca_ref[pl.ds(qnb, (CONV_A - 2) * qnb)]
        o_ca_ref[pl.ds((CONV_A - 2) * qnb, qnb)] = ua_buf[...]
        o_cb_ref[pl.ds(0, (CONV_B - 2) * qnb)] = st_cb_ref[pl.ds(qnb, (CONV_B - 2) * qnb)]
        o_cb_ref[pl.ds((CONV_B - 2) * qnb, qnb)] = gb_buf[...]
    else:
        new_ca = ua_buf[pl.ds(tt * qnb, hist_a)]
        new_cb = gb_buf[pl.ds(tt * qnb, hist_b)]
        ua_buf[pl.ds(0, hist_a)] = new_ca
        gb_buf[pl.ds(0, hist_b)] = new_cb

    @pl.when(step == n_steps - 1)
    def _():
        if not single_step:
            o_ca_ref[...] = ua_buf[pl.ds(0, hist_a)]
            o_cb_ref[...] = gb_buf[pl.ds(0, hist_b)]
        o_h_ref[...] = h_st[...]
        o_sr_ref[...] = sr_st[...]
        o_si_ref[...] = si_st[...]


def _ffn_kernel(x_ref, g_ffn_ref, w_fi_ref, w_fo_ref, g_fin_ref, o_ref, act_buf, *, final_norm):
    x = x_ref[...]
    h2 = _rms_norm(x, g_ffn_ref[...]).astype(BF16)
    for c in range(D_FF // V7X_MXU_DIM):
        lo = c * V7X_MXU_DIM
        gate = _dot(h2, w_fi_ref[:, lo:lo + V7X_MXU_DIM])
        up = _dot(h2, w_fi_ref[:, D_FF + lo:D_FF + lo + V7X_MXU_DIM])
        act_buf[:, lo:lo + V7X_MXU_DIM] = (_silu(gate) * up).astype(BF16)
    y = x + _dot(act_buf[...], w_fo_ref[...])
    if final_norm:
        y = _rms_norm(y, g_fin_ref[...])
    o_ref[...] = y


def _s5_prep_kernel(lam_re_ref, lam_im_ref, log_dt_ref, b_re_ref, b_im_ref,
                    ar_ref, ai_ref, bbr_ref, bbi_ref):
    dt = jnp.exp(log_dt_ref[...])
    lr, li = lam_re_ref[...], lam_im_ref[...]
    mag = jnp.exp(lr * dt)
    ar, ai = mag * jnp.cos(li * dt), mag * jnp.sin(li * dt)
    den = lr * lr + li * li
    qr = ((ar - 1.0) * lr + ai * li) / den
    qi = (ai * lr - (ar - 1.0) * li) / den
    br, bi = b_re_ref[...], b_im_ref[...]
    ar_ref[...] = ar
    ai_ref[...] = ai
    bbr_ref[...] = qr * br - qi * bi
    bbi_ref[...] = qr * bi + qi * br


def _full_spec(shape, single_buffer=False):
    idx = lambda i, _n=len(shape): (0,) * _n
    if single_buffer:
        return pl.BlockSpec(shape, idx, pipeline_mode=pl.Buffered(1))
    return pl.BlockSpec(shape, idx)


def _nbytes(shape, dtype):
    return math.prod(shape) * jnp.dtype(dtype).itemsize


def _mixer_call(x, states, weights, *, nb, tt):
    n_rows = x.shape[0]
    rows = nb * tt
    n_steps = n_rows // rows
    assert n_steps * rows == n_rows and nb % V7X_SUBLANES == 0
    assert (tt == 1 and n_steps == 1) or tt >= CONV_B - 1
    qnb = nb // V7X_SUBLANES
    hist_steps = 0 if tt == 1 else 1
    x_spec = pl.BlockSpec((rows, D_MODEL), lambda i: (i, 0))
    st_shapes = [((CONV_A - 1) * qnb, V7X_SUBLANES, W_A), (nb, W_A),
                 ((CONV_B - 1) * qnb, V7X_SUBLANES, W_B), (nb, S_C), (nb, S_C)]
    scratch = [
        ((hist_steps * (CONV_A - 1) + tt) * qnb, V7X_SUBLANES, W_A), (tt * qnb, V7X_SUBLANES, W_A),
        ((hist_steps * (CONV_B - 1) + tt) * qnb, V7X_SUBLANES, W_B), (tt * qnb, V7X_SUBLANES, W_B),
        (rows, W_A), (rows, W_A), (rows, S_C), (rows, S_C),
        (nb, W_A), (nb, S_C), (nb, S_C)]
    w_bytes = sum(_nbytes(w.shape, w.dtype) for w in weights)
    est = (w_bytes + 4 * _nbytes((rows, D_MODEL), F32)
           + 3 * sum(_nbytes(s, F32) for s in st_shapes)
           + sum(_nbytes(s, F32) for s in scratch)
           + 12 * _nbytes((rows, D_MODEL), F32))
    out = pl.pallas_call(
        functools.partial(_mixer_kernel, nb=nb, tt=tt, n_steps=n_steps),
        grid=(n_steps,),
        in_specs=[x_spec] + [_full_spec(s, single_buffer=True) for s in st_shapes]
                 + [_full_spec(w.shape, single_buffer=True) for w in weights],
        out_specs=[x_spec] + [_full_spec(s) for s in st_shapes],
        out_shape=[jax.ShapeDtypeStruct((n_rows, D_MODEL), F32)]
                  + [jax.ShapeDtypeStruct(s, F32) for s in st_shapes],
        scratch_shapes=[pltpu.VMEM(s, F32) for s in scratch],
        compiler_params=pltpu.CompilerParams(
            dimension_semantics=("arbitrary",),
            vmem_limit_bytes=min(est, V7X_VMEM_BYTES - 6 * 1024 * 1024)),
        name="mixer_nb%d" % nb,
    )(x, *states, *weights)
    return out[0], out[1:]


def _ffn_call(x, g_ffn, w_fi, w_fo, g_fin, *, rows, final_norm):
    n_rows = x.shape[0]
    n_steps = n_rows // rows
    assert n_steps * rows == n_rows
    x_spec = pl.BlockSpec((rows, D_MODEL), lambda i: (i, 0))
    weights = (g_ffn, w_fi, w_fo, g_fin)
    est = (sum(_nbytes(w.shape, w.dtype) for w in weights) + 4 * _nbytes((rows, D_MODEL), F32)
           + _nbytes((rows, D_FF), BF16) + 8 * _nbytes((rows, D_MODEL), F32))
    return pl.pallas_call(
        functools.partial(_ffn_kernel, final_norm=final_norm),
        grid=(n_steps,),
        in_specs=[x_spec] + [_full_spec(w.shape, single_buffer=True) for w in weights],
        out_specs=x_spec,
        out_shape=jax.ShapeDtypeStruct((n_rows, D_MODEL), F32),
        scratch_shapes=[pltpu.VMEM((rows, D_FF), BF16)],
        compiler_params=pltpu.CompilerParams(
            dimension_semantics=("arbitrary",),
            vmem_limit_bytes=min(est, V7X_VMEM_BYTES - 6 * 1024 * 1024)),
        name="ffn_r%d" % rows,
    )(x, *weights)


def _s5_prep(lam_re, lam_im, log_dt, b_ssm_re, b_ssm_im):
    n_l = lam_re.shape[0]
    rep = lambda a: jnp.repeat(a.reshape(n_l * G_C, -1), GS_C, axis=0)
    lam_re_r, lam_im_r = rep(lam_re), rep(lam_im)
    log_dt_r = jnp.broadcast_to(rep(log_dt), (n_l * G_C * GS_C, P_C))
    to_rows = lambda b: b.transpose(0, 1, 3, 2).reshape(n_l * G_C * GS_C, P_C)
    shp = jax.ShapeDtypeStruct((n_l * G_C * GS_C, P_C), F32)
    ar, ai, bbr, bbi = pl.pallas_call(_s5_prep_kernel, out_shape=[shp] * 4, name="s5_prep")(
        lam_re_r, lam_im_r, log_dt_r, to_rows(b_ssm_re), to_rows(b_ssm_im))
    first = lambda a: a.reshape(n_l, G_C, GS_C, P_C)[:, :, 0, :].reshape(n_l, 1, S_C)
    return first(ar), first(ai), bbr.reshape(n_l, G_C, GS_C, P_C), bbi.reshape(n_l, G_C, GS_C, P_C)


def _layer_weights(l, p, ar, ai, bbr, bbi):
    gpc = V7X_LANES // GS_C
    eye = jnp.eye(gpc, dtype=F32)

    def in_blockdiag(bb):
        b4 = bb.reshape(S5_CHUNKS, gpc, GS_C, P_C)
        return jnp.einsum('jgcp,gh->jgchp', b4, eye).reshape(S5_CHUNKS, V7X_LANES, gpc * P_C)

    def out_blockdiag(cc):
        c4 = cc.reshape(S5_CHUNKS, gpc, GS_C, P_C)
        return jnp.einsum('jgcp,gh->jgphc', c4, eye).reshape(S5_CHUNKS, gpc * P_C, V7X_LANES)

    bmat = jnp.concatenate([in_blockdiag(bbr[l]), in_blockdiag(bbi[l])], axis=-1).astype(BF16)
    row = lambda a: a[l].reshape(1, -1)
    bc8 = lambda a: jnp.broadcast_to(a[l][:, None, :], (a.shape[1], V7X_SUBLANES, a.shape[2]))
    mixer = (
        row(p['g_mix']), p['w_in'][l].astype(BF16), bc8(p['w_conv_a']), row(p['b_conv_a']),
        jnp.concatenate([p['w_rg'][l], p['w_ig'][l]], axis=-1).astype(BF16),
        row(p['b_rg']), row(p['b_ig']), row(p['lam_a']),
        bc8(p['w_dw_b']), row(p['b_dw_b']), row(p['ln_g_b']), row(p['ln_b_b']),
        ar[l], ai[l], bmat,
        out_blockdiag(p['c_ssm_re'][l]).astype(BF16), out_blockdiag(p['c_ssm_im'][l]).astype(BF16),
        row(p['d_ssm']), p['w_glu_c'][l].astype(BF16), row(p['b_glu_c']),
        row(p['b_gate']), p['w_pa'][l].astype(BF16), p['w_pb'][l].astype(BF16),
        p['w_pc'][l].astype(BF16), p['w_out'][l].astype(BF16))
    ffn = (row(p['g_ffn']), p['w_ffn_in'][l].astype(BF16), p['w_ffn_out'][l].astype(BF16))
    return mixer, ffn


def _trunk(x, states, layer_w, g_final, *, nb, tt, ffn_rows):
    new_states = []
    for l in range(DEPTH):
        mixer_w, ffn_w = layer_w[l]
        x, st = _mixer_call(x, states[l], mixer_w, nb=nb, tt=tt)
        x = _ffn_call(x, *ffn_w, g_final, rows=ffn_rows, final_norm=(l == DEPTH - 1))
        new_states.append(st)
    return x, new_states


def _conv_state_in(s, nb):
    k1, c = s.shape[1], s.shape[2]
    return s.transpose(1, 0, 2).reshape(k1 * nb // V7X_SUBLANES, V7X_SUBLANES, c)


def _conv_state_out(s, nb):
    c = s.shape[-1]
    return s.reshape(-1, nb, c).transpose(1, 0, 2)


def kernel(x_prompt, x_sample, state_lru_conv, state_lru_h, state_cfm_conv, state_ssm_re, state_ssm_im, g_mix, w_in, w_conv_a, b_conv_a, w_rg, b_rg, w_ig, b_ig, lam_a, w_dw_b, b_dw_b, ln_g_b, ln_b_b, lam_re, lam_im, log_dt, b_ssm_re, b_ssm_im, c_ssm_re, c_ssm_im, d_ssm, w_glu_c, b_glu_c, b_gate, w_pa, w_pb, w_pc, w_out, g_ffn, w_ffn_in, w_ffn_out, g_final):
    p = dict(g_mix=g_mix, w_in=w_in, w_conv_a=w_conv_a, b_conv_a=b_conv_a, w_rg=w_rg, b_rg=b_rg,
             w_ig=w_ig, b_ig=b_ig, lam_a=lam_a, w_dw_b=w_dw_b, b_dw_b=b_dw_b, ln_g_b=ln_g_b,
             ln_b_b=ln_b_b, c_ssm_re=c_ssm_re, c_ssm_im=c_ssm_im, d_ssm=d_ssm, w_glu_c=w_glu_c,
             b_glu_c=b_glu_c, b_gate=b_gate, w_pa=w_pa, w_pb=w_pb, w_pc=w_pc, w_out=w_out,
             g_ffn=g_ffn, w_ffn_in=w_ffn_in, w_ffn_out=w_ffn_out)
    ar, ai, bbr, bbi = _s5_prep(lam_re, lam_im, log_dt, b_ssm_re, b_ssm_im)
    layer_w = [_layer_weights(l, p, ar, ai, bbr, bbi) for l in range(DEPTH)]
    g_fin = g_final.reshape(1, D_MODEL)

    n_p, t_p, _ = x_prompt.shape
    xp = x_prompt.transpose(1, 0, 2).reshape(t_p * n_p, D_MODEL)
    zero_states = [(jnp.zeros(((CONV_A - 1) * n_p // V7X_SUBLANES, V7X_SUBLANES, W_A), F32),
                    jnp.zeros((n_p, W_A), F32),
                    jnp.zeros(((CONV_B - 1) * n_p // V7X_SUBLANES, V7X_SUBLANES, W_B), F32),
                    jnp.zeros((n_p, S_C), F32), jnp.zeros((n_p, S_C), F32))] * DEPTH
    yp, p_new = _trunk(xp, zero_states, layer_w, g_fin, nb=n_p, tt=32, ffn_rows=512)
    y_prompt = yp.reshape(t_p, n_p, D_MODEL).transpose(1, 0, 2)

    n_s, t_s, _ = x_sample.shape
    xs = x_sample.transpose(1, 0, 2).reshape(t_s * n_s, D_MODEL)
    s_states = [(_conv_state_in(state_lru_conv[l], n_s), state_lru_h[l],
                 _conv_state_in(state_cfm_conv[l], n_s),
                 state_ssm_re[l].reshape(n_s, S_C), state_ssm_im[l].reshape(n_s, S_C))
                for l in range(DEPTH)]
    ys, s_new = _trunk(xs, s_states, layer_w, g_fin, nb=n_s, tt=t_s, ffn_rows=n_s * t_s)
    y_sample = ys.reshape(t_s, n_s, D_MODEL).transpose(1, 0, 2)

    def unpack(new, nb):
        stack = lambda i, f: jnp.stack([f(new[l][i]) for l in range(DEPTH)], axis=0)
        return (stack(0, lambda s: _conv_state_out(s, nb)), stack(1, lambda s: s),
                stack(2, lambda s: _conv_state_out(s, nb)),
                stack(3, lambda s: s.reshape(nb, G_C, P_C)),
                stack(4, lambda s: s.reshape(nb, G_C, P_C)))

    return (y_prompt, y_sample) + unpack(p_new, n_p) + unpack(s_new, n_s)
```

```python
import functools
import math

import jax
import jax.numpy as jnp
from jax import lax
from jax.experimental import pallas as pl
from jax.experimental.pallas import tpu as pltpu

D_MODEL = 1024
DEPTH = 2
W_A = D_MODEL
N_BLOCKS_A = 8
BLK_A = W_A // N_BLOCKS_A
CONV_A = 4
LRU_C = 8.0
W_B = D_MODEL // 2
CONV_B = 31
W_C = D_MODEL // 2
GS_C = 16
G_C = W_C // GS_C
P_C = 64
S_C = G_C * P_C
N_BRANCH = 3
IN_W = W_A + 2 * W_B + W_C + N_BRANCH * D_MODEL
D_FF = int(math.ceil(8 * D_MODEL / 3 / 256) * 256)
EPS = 1e-6

V7X_SUBLANES = 8
V7X_LANES = 128
V7X_MXU_DIM = 256
V7X_VMEM_BYTES = 64 * 1024 * 1024

OFF_ZB = W_A
OFF_UC = W_A + 2 * W_B
OFF_GATE = W_A + 2 * W_B + W_C
S5_LANE_GROUP = 4 * V7X_LANES
S5_CHUNKS = W_C // V7X_LANES
S5_CHUNK_STATE = S_C // S5_CHUNKS
CONV_TILE_BLOCK = 4

BF16 = jnp.bfloat16
F32 = jnp.float32


def _sigmoid(x):
    return 1.0 / (1.0 + jnp.exp(-x))


def _silu(x):
    return x * _sigmoid(x)


def _gelu_tanh(x):
    c = math.sqrt(2.0 / math.pi)
    return 0.5 * x * (1.0 + jnp.tanh(c * (x + 0.044715 * (x * x * x))))


def _softplus(x):
    return jnp.maximum(x, 0.0) + jnp.log1p(jnp.exp(-jnp.abs(x)))


def _rms_norm(x, g):
    return x * lax.rsqrt(jnp.mean(x * x, axis=-1, keepdims=True) + EPS) * g


def _dot(a, b):
    return jnp.dot(a, b, preferred_element_type=F32)


def _dwconv(src, w_ref, bias_ref, out_ref, taps, qnb, n_out, tile_block):
    c = out_ref.shape[-1]
    for lc in range(c // V7X_LANES):
        lanes = slice(lc * V7X_LANES, (lc + 1) * V7X_LANES)
        bias = jnp.broadcast_to(bias_ref[:, lanes], (V7X_SUBLANES, V7X_LANES))
        for q0 in range(0, n_out, tile_block):
            loaded = {}
            for q in range(q0, min(q0 + tile_block, n_out)):
                acc = bias
                for k in range(taps):
                    i = q + k * qnb
                    if i not in loaded:
                        loaded[i] = src(i, lanes)
                    acc = acc + w_ref[k, :, lanes] * loaded[i]
                out_ref[q, :, lanes] = acc


def _conv_src(hist_ref, buf_ref, hist_tiles, single_step):
    if single_step:
        def src(i, lanes):
            if i < hist_tiles:
                return hist_ref[i, :, lanes]
            return buf_ref[i - hist_tiles, :, lanes]
    else:
        def src(i, lanes):
            return buf_ref[i, :, lanes]
    return src


def _mixer_kernel(
        x_ref, st_ca_ref, st_h_ref, st_cb_ref, st_sr_ref, st_si_ref,
        g_mix_ref, w_in_ref, wca_ref, bca_ref, wrgig_ref, b_rg_ref, b_ig_ref, lam_a_ref,
        wdb_ref, bdb_ref, ln_g_ref, ln_b_ref,
        ar_ref, ai_ref, bmat_ref, cre_ref, cim_ref, d_ssm_ref, w_glu_ref, b_glu_ref,
        b_gate_ref, w_pa_ref, w_pb_ref, w_pc_ref, w_out_ref,
        xo_ref, o_ca_ref, o_h_ref, o_cb_ref, o_sr_ref, o_si_ref,
        ua_buf, ca_buf, gb_buf, cb_buf, a_buf, bx_buf, xr_buf, xi_buf, h_st, sr_st, si_st,
        *, nb, tt, n_steps):
    rows = nb * tt
    qnb = nb // V7X_SUBLANES
    qrows = rows // V7X_SUBLANES
    step = pl.program_id(0)
    single_step = tt == 1
    hist_a = 0 if single_step else (CONV_A - 1) * qnb
    hist_b = 0 if single_step else (CONV_B - 1) * qnb

    @pl.when(step == 0)
    def _():
        if not single_step:
            ua_buf[pl.ds(0, hist_a)] = st_ca_ref[...]
            gb_buf[pl.ds(0, hist_b)] = st_cb_ref[...]
        h_st[...] = st_h_ref[...]
        sr_st[...] = st_sr_ref[...]
        si_st[...] = st_si_ref[...]

    x = x_ref[...]
    h = _rms_norm(x, g_mix_ref[...]).astype(BF16)

    u_a = _dot(h, w_in_ref[:, 0:W_A])
    ua_buf[pl.ds(hist_a, qrows)] = u_a.reshape(qrows, V7X_SUBLANES, W_A)
    _dwconv(_conv_src(st_ca_ref, ua_buf, (CONV_A - 1) * qnb, single_step), wca_ref, bca_ref,
            ca_buf, CONV_A, qnb, qrows, CONV_TILE_BLOCK)
    neg_c_softplus = -LRU_C * _softplus(-lam_a_ref[...])
    for i in range(N_BLOCKS_A):
        sl = slice(i * BLK_A, (i + 1) * BLK_A)
        c_blk = ca_buf[:, :, sl].reshape(rows, BLK_A)
        pre = _dot(c_blk.astype(BF16), wrgig_ref[i])
        r = _sigmoid(pre[:, :BLK_A] + b_rg_ref[:, sl])
        ig = _sigmoid(pre[:, BLK_A:] + b_ig_ref[:, sl])
        log_a = r * neg_c_softplus[:, sl]
        a = jnp.exp(log_a)
        a_buf[:, sl] = a
        bx_buf[:, sl] = jnp.sqrt(-jnp.tanh(log_a) * (a * a + 1.0)) * ig * c_blk

    h_last = h_st[...]
    for t in range(tt):
        h_last = a_buf[t * nb:(t + 1) * nb, :] * h_last + bx_buf[t * nb:(t + 1) * nb, :]
        bx_buf[t * nb:(t + 1) * nb, :] = h_last
    h_st[...] = h_last
    pa = _dot(bx_buf[...].astype(BF16), w_pa_ref[...])
    merged = _sigmoid(_dot(h, w_in_ref[:, OFF_GATE:OFF_GATE + D_MODEL])
                      + b_gate_ref[:, 0:D_MODEL]) * pa

    z_b = _dot(h, w_in_ref[:, OFF_ZB:OFF_ZB + 2 * W_B])
    glu_b = z_b[:, :W_B] * _sigmoid(z_b[:, W_B:])
    gb_buf[pl.ds(hist_b, qrows)] = glu_b.reshape(qrows, V7X_SUBLANES, W_B)
    _dwconv(_conv_src(st_cb_ref, gb_buf, (CONV_B - 1) * qnb, single_step), wdb_ref, bdb_ref,
            cb_buf, CONV_B, qnb, qrows, CONV_TILE_BLOCK)
    c_b = cb_buf[...].reshape(rows, W_B)
    mu = jnp.mean(c_b, axis=-1, keepdims=True)
    xc = c_b - mu
    ln = xc * lax.rsqrt(jnp.mean(xc * xc, axis=-1, keepdims=True) + EPS)
    y_b = _silu(ln * ln_g_ref[...] + ln_b_ref[...])
    pb = _dot(y_b.astype(BF16), w_pb_ref[...])
    merged = merged + _sigmoid(_dot(h, w_in_ref[:, OFF_GATE + D_MODEL:OFF_GATE + 2 * D_MODEL])
                               + b_gate_ref[:, D_MODEL:2 * D_MODEL]) * pb

    u_c = _dot(h, w_in_ref[:, OFF_UC:OFF_UC + W_C])
    for j in range(S5_CHUNKS):
        xcat = _dot(u_c[:, j * V7X_LANES:(j + 1) * V7X_LANES].astype(BF16), bmat_ref[j])
        sl = slice(j * S5_CHUNK_STATE, (j + 1) * S5_CHUNK_STATE)
        xr_buf[:, sl] = xcat[:, :S5_CHUNK_STATE]
        xi_buf[:, sl] = xcat[:, S5_CHUNK_STATE:]
    for g4 in range(S_C // S5_LANE_GROUP):
        sl = slice(g4 * S5_LANE_GROUP, (g4 + 1) * S5_LANE_GROUP)
        ar_b = jnp.broadcast_to(ar_ref[:, sl], (nb, S5_LANE_GROUP))
        ai_b = jnp.broadcast_to(ai_ref[:, sl], (nb, S5_LANE_GROUP))
        nsr, nsi = sr_st[:, sl], si_st[:, sl]
        for t in range(tt):
            rs = slice(t * nb, (t + 1) * nb)
            nsr, nsi = (ar_b * nsr - ai_b * nsi + xr_buf[rs, sl],
                        ar_b * nsi + ai_b * nsr + xi_buf[rs, sl])
            xr_buf[rs, sl] = nsr
            xi_buf[rs, sl] = nsi
        sr_st[:, sl] = nsr
        si_st[:, sl] = nsi
    y_parts = []
    for j in range(S5_CHUNKS):
        sl = slice(j * S5_CHUNK_STATE, (j + 1) * S5_CHUNK_STATE)
        y_parts.append(_dot(xr_buf[:, sl].astype(BF16), cre_ref[j])
                       - _dot(xi_buf[:, sl].astype(BF16), cim_ref[j]))
    y_c = jnp.concatenate(y_parts, axis=-1) + d_ssm_ref[...] * u_c
    y_c = _gelu_tanh(y_c)
    y_c = y_c * _sigmoid(_dot(y_c.astype(BF16), w_glu_ref[...]) + b_glu_ref[...])
    pc = _dot(y_c.astype(BF16), w_pc_ref[...])
    merged = merged + _sigmoid(_dot(h, w_in_ref[:, OFF_GATE + 2 * D_MODEL:OFF_GATE + 3 * D_MODEL])
                               + b_gate_ref[:, 2 * D_MODEL:3 * D_MODEL]) * pc

    xo_ref[...] = x + _dot(merged.astype(BF16), w_out_ref[...])

    if single_step:
        o_ca_ref[pl.ds(0, (CONV_A - 2) * qnb)] = st_ca_ref[pl.ds(qnb, (CONV_A - 2) * qnb)]
        o_ca_ref[pl.ds((CONV_A - 2) * qnb, qnb)] = ua_buf[...]
        o_cb_ref[pl.ds(0, (CONV_B - 2) * qnb)] = st_cb_ref[pl.ds(qnb, (CONV_B - 2) * qnb)]
        o_cb_ref[pl.ds((CONV_B - 2) * qnb, qnb)] = gb_buf[...]
    else:
        new_ca = ua_buf[pl.ds(tt * qnb, hist_a)]
        new_cb = gb_buf[pl.ds(tt * qnb, hist_b)]
        ua_buf[pl.ds(0, hist_a)] = new_ca
        gb_buf[pl.ds(0, hist_b)] = new_cb

    @pl.when(step == n_steps - 1)
    def _():
        if not single_step:
            o_ca_ref[...] = ua_buf[pl.ds(0, hist_a)]
            o_cb_ref[...] = gb_buf[pl.ds(0, hist_b)]
        o_h_ref[...] = h_st[...]
        o_sr_ref[...] = sr_st[...]
        o_si_ref[...] = si_st[...]


def _ffn_kernel(x_ref, g_ffn_ref, w_fi_ref, w_fo_ref, g_fin_ref, o_ref, act_buf, *, final_norm):
    x = x_ref[...]
    h2 = _rms_norm(x, g_ffn_ref[...]).astype(BF16)
    for c in range(D_FF // V7X_MXU_DIM):
        lo = c * V7X_MXU_DIM
        gate = _dot(h2, w_fi_ref[:, lo:lo + V7X_MXU_DIM])
        up = _dot(h2, w_fi_ref[:, D_FF + lo:D_FF + lo + V7X_MXU_DIM])
        act_buf[:, lo:lo + V7X_MXU_DIM] = (_silu(gate) * up).astype(BF16)
    y = x + _dot(act_buf[...], w_fo_ref[...])
    if final_norm:
        y = _rms_norm(y, g_fin_ref[...])
    o_ref[...] = y


def _s5_prep_kernel(lam_re_ref, lam_im_ref, log_dt_ref, b_re_ref, b_im_ref,
                    ar_ref, ai_ref, bbr_ref, bbi_ref):
    dt = jnp.exp(log_dt_ref[...])
    lr, li = lam_re_ref[...], lam_im_ref[...]
    mag = jnp.exp(lr * dt)
    ar, ai = mag * jnp.cos(li * dt), mag * jnp.sin(li * dt)
    den = lr * lr + li * li
    qr = ((ar - 1.0) * lr + ai * li) / den
    qi = (ai * lr - (ar - 1.0) * li) / den
    br, bi = b_re_ref[...], b_im_ref[...]
    ar_ref[...] = ar
    ai_ref[...] = ai
    bbr_ref[...] = qr * br - qi * bi
    bbi_ref[...] = qr * bi + qi * br


def _full_spec(shape, single_buffer=False):
    idx = lambda i, _n=len(shape): (0,) * _n
    if single_buffer:
        return pl.BlockSpec(shape, idx, pipeline_mode=pl.Buffered(1))
    return pl.BlockSpec(shape, idx)


def _nbytes(shape, dtype):
    return math.prod(shape) * jnp.dtype(dtype).itemsize


def _mixer_call(x, states, weights, *, nb, tt):
    n_rows = x.shape[0]
    rows = nb * tt
    n_steps = n_rows // rows
    assert n_steps * rows == n_rows and nb % V7X_SUBLANES == 0
    assert (tt == 1 and n_steps == 1) or tt >= CONV_B - 1
    qnb = nb // V7X_SUBLANES
    hist_steps = 0 if tt == 1 else 1
    x_spec = pl.BlockSpec((rows, D_MODEL), lambda i: (i, 0))
    st_shapes = [((CONV_A - 1) * qnb, V7X_SUBLANES, W_A), (nb, W_A),
                 ((CONV_B - 1) * qnb, V7X_SUBLANES, W_B), (nb, S_C), (nb, S_C)]
    scratch = [
        ((hist_steps * (CONV_A - 1) + tt) * qnb, V7X_SUBLANES, W_A), (tt * qnb, V7X_SUBLANES, W_A),
        ((hist_steps * (CONV_B - 1) + tt) * qnb, V7X_SUBLANES, W_B), (tt * qnb, V7X_SUBLANES, W_B),
        (rows, W_A), (rows, W_A), (rows, S_C), (rows, S_C),
        (nb, W_A), (nb, S_C), (nb, S_C)]
    w_bytes = sum(_nbytes(w.shape, w.dtype) for w in weights)
    est = (w_bytes + 4 * _nbytes((rows, D_MODEL), F32)
           + 3 * sum(_nbytes(s, F32) for s in st_shapes)
           + sum(_nbytes(s, F32) for s in scratch)
           + 12 * _nbytes((rows, D_MODEL), F32))
    out = pl.pallas_call(
        functools.partial(_mixer_kernel, nb=nb, tt=tt, n_steps=n_steps),
        grid=(n_steps,),
        in_specs=[x_spec] + [_full_spec(s, single_buffer=True) for s in st_shapes]
                 + [_full_spec(w.shape, single_buffer=True) for w in weights],
        out_specs=[x_spec] + [_full_spec(s) for s in st_shapes],
        out_shape=[jax.ShapeDtypeStruct((n_rows, D_MODEL), F32)]
                  + [jax.ShapeDtypeStruct(s, F32) for s in st_shapes],
        scratch_shapes=[pltpu.VMEM(s, F32) for s in scratch],
        compiler_params=pltpu.CompilerParams(
            dimension_semantics=("arbitrary",),
            vmem_limit_bytes=min(est, V7X_VMEM_BYTES - 6 * 1024 * 1024)),
        name="mixer_nb%d" % nb,
    )(x, *states, *weights)
    return out[0], out[1:]


def _ffn_call(x, g_ffn, w_fi, w_fo, g_fin, *, rows, final_norm):
    n_rows = x.shape[0]
    n_steps = n_rows // rows
    assert n_steps * rows == n_rows
    x_spec = pl.BlockSpec((rows, D_MODEL), lambda i: (i, 0))
    weights = (g_ffn, w_fi, w_fo, g_fin)
    est = (sum(_nbytes(w.shape, w.dtype) for w in weights) + 4 * _nbytes((rows, D_MODEL), F32)
           + _nbytes((rows, D_FF), BF16) + 8 * _nbytes((rows, D_MODEL), F32))
    return pl.pallas_call(
        functools.partial(_ffn_kernel, final_norm=final_norm),
        grid=(n_steps,),
        in_specs=[x_spec] + [_full_spec(w.shape, single_buffer=True) for w in weights],
        out_specs=x_spec,
        out_shape=jax.ShapeDtypeStruct((n_rows, D_MODEL), F32),
        scratch_shapes=[pltpu.VMEM((rows, D_FF), BF16)],
        compiler_params=pltpu.CompilerParams(
            dimension_semantics=("arbitrary",),
            vmem_limit_bytes=min(est, V7X_VMEM_BYTES - 6 * 1024 * 1024)),
        name="ffn_r%d" % rows,
    )(x, *weights)


def _s5_prep(lam_re, lam_im, log_dt, b_ssm_re, b_ssm_im):
    n_l = lam_re.shape[0]
    rep = lambda a: jnp.repeat(a.reshape(n_l * G_C, -1), GS_C, axis=0)
    lam_re_r, lam_im_r = rep(lam_re), rep(lam_im)
    log_dt_r = jnp.broadcast_to(rep(log_dt), (n_l * G_C * GS_C, P_C))
    to_rows = lambda b: b.transpose(0, 1, 3, 2).reshape(n_l * G_C * GS_C, P_C)
    shp = jax.ShapeDtypeStruct((n_l * G_C * GS_C, P_C), F32)
    ar, ai, bbr, bbi = pl.pallas_call(_s5_prep_kernel, out_shape=[shp] * 4, name="s5_prep")(
        lam_re_r, lam_im_r, log_dt_r, to_rows(b_ssm_re), to_rows(b_ssm_im))
    first = lambda a: a.reshape(n_l, G_C, GS_C, P_C)[:, :, 0, :].reshape(n_l, 1, S_C)
    return first(ar), first(ai), bbr.reshape(n_l, G_C, GS_C, P_C), bbi.reshape(n_l, G_C, GS_C, P_C)


def _layer_weights(l, p, ar, ai, bbr, bbi):
    gpc = V7X_LANES // GS_C
    eye = jnp.eye(gpc, dtype=F32)

    def in_blockdiag(bb):
        b4 = bb.reshape(S5_CHUNKS, gpc, GS_C, P_C)
        return jnp.einsum('jgcp,gh->jgchp', b4, eye).reshape(S5_CHUNKS, V7X_LANES, gpc * P_C)

    def out_blockdiag(cc):
        c4 = cc.reshape(S5_CHUNKS, gpc, GS_C, P_C)
        return jnp.einsum('jgcp,gh->jgphc', c4, eye).reshape(S5_CHUNKS, gpc * P_C, V7X_LANES)

    bmat = jnp.concatenate([in_blockdiag(bbr[l]), in_blockdiag(bbi[l])], axis=-1).astype(BF16)
    row = lambda a: a[l].reshape(1, -1)
    bc8 = lambda a: jnp.broadcast_to(a[l][:, None, :], (a.shape[1], V7X_SUBLANES, a.shape[2]))
    mixer = (
        row(p['g_mix']), p['w_in'][l].astype(BF16), bc8(p['w_conv_a']), row(p['b_conv_a']),
        jnp.concatenate([p['w_rg'][l], p['w_ig'][l]], axis=-1).astype(BF16),
        row(p['b_rg']), row(p['b_ig']), row(p['lam_a']),
        bc8(p['w_dw_b']), row(p['b_dw_b']), row(p['ln_g_b']), row(p['ln_b_b']),
        ar[l], ai[l], bmat,
        out_blockdiag(p['c_ssm_re'][l]).astype(BF16), out_blockdiag(p['c_ssm_im'][l]).astype(BF16),
        row(p['d_ssm']), p['w_glu_c'][l].astype(BF16), row(p['b_glu_c']),
        row(p['b_gate']), p['w_pa'][l].astype(BF16), p['w_pb'][l].astype(BF16),
        p['w_pc'][l].astype(BF16), p['w_out'][l].astype(BF16))
    ffn = (row(p['g_ffn']), p['w_ffn_in'][l].astype(BF16), p['w_ffn_out'][l].astype(BF16))
    return mixer, ffn


def _trunk(x, states, layer_w, g_final, *, nb, tt, ffn_rows):
    new_states = []
    for l in range(DEPTH):
        mixer_w, ffn_w = layer_w[l]
        x, st = _mixer_call(x, states[l], mixer_w, nb=nb, tt=tt)
        x = _ffn_call(x, *ffn_w, g_final, rows=ffn_rows, final_norm=(l == DEPTH - 1))
        new_states.append(st)
    return x, new_states


def _conv_state_in(s, nb):
    k1, c = s.shape[1], s.shape[2]
    return s.transpose(1, 0, 2).reshape(k1 * nb // V7X_SUBLANES, V7X_SUBLANES, c)


def _conv_state_out(s, nb):
    c = s.shape[-1]
    return s.reshape(-1, nb, c).transpose(1, 0, 2)


def kernel(x_prompt, x_sample, state_lru_conv, state_lru_h, state_cfm_conv, state_ssm_re, state_ssm_im, g_mix, w_in, w_conv_a, b_conv_a, w_rg, b_rg, w_ig, b_ig, lam_a, w_dw_b, b_dw_b, ln_g_b, ln_b_b, lam_re, lam_im, log_dt, b_ssm_re, b_ssm_im, c_ssm_re, c_ssm_im, d_ssm, w_glu_c, b_glu_c, b_gate, w_pa, w_pb, w_pc, w_out, g_ffn, w_ffn_in, w_ffn_out, g_final):
    p = dict(g_mix=g_mix, w_in=w_in, w_conv_a=w_conv_a, b_conv_a=b_conv_a, w_rg=w_rg, b_rg=b_rg,
             w_ig=w_ig, b_ig=b_ig, lam_a=lam_a, w_dw_b=w_dw_b, b_dw_b=b_dw_b, ln_g_b=ln_g_b,
             ln_b_b=ln_b_b, c_ssm_re=c_ssm_re, c_ssm_im=c_ssm_im, d_ssm=d_ssm, w_glu_c=w_glu_c,
             b_glu_c=b_glu_c, b_gate=b_gate, w_pa=w_pa, w_pb=w_pb, w_pc=w_pc, w_out=w_out,
             g_ffn=g_ffn, w_ffn_in=w_ffn_in, w_ffn_out=w_ffn_out)
    ar, ai, bbr, bbi = _s5_prep(lam_re, lam_im, log_dt, b_ssm_re, b_ssm_im)
    layer_w = [_layer_weights(l, p, ar, ai, bbr, bbi) for l in range(DEPTH)]
    g_fin = g_final.reshape(1, D_MODEL)

    n_p, t_p, _ = x_prompt.shape
    xp = x_prompt.transpose(1, 0, 2).reshape(t_p * n_p, D_MODEL)
    zero_states = [(jnp.zeros(((CONV_A - 1) * n_p // V7X_SUBLANES, V7X_SUBLANES, W_A), F32),
                    jnp.zeros((n_p, W_A), F32),
                    jnp.zeros(((CONV_B - 1) * n_p // V7X_SUBLANES, V7X_SUBLANES, W_B), F32),
                    jnp.zeros((n_p, S_C), F32), jnp.zeros((n_p, S_C), F32))] * DEPTH
    yp, p_new = _trunk(xp, zero_states, layer_w, g_fin, nb=n_p, tt=32, ffn_rows=512)
    y_prompt = yp.reshape(t_p, n_p, D_MODEL).transpose(1, 0, 2)

    n_s, t_s, _ = x_sample.shape
    xs = x_sample.transpose(1, 0, 2).reshape(t_s * n_s, D_MODEL)
    s_states = [(_conv_state_in(state_lru_conv[l], n_s), state_lru_h[l],
                 _conv_state_in(state_cfm_conv[l], n_s),
                 state_ssm_re[l].reshape(n_s, S_C), state_ssm_im[l].reshape(n_s, S_C))
                for l in range(DEPTH)]
    ys, s_new = _trunk(xs, s_states, layer_w, g_fin, nb=n_s, tt=t_s, ffn_rows=n_s * t_s)
    y_sample = ys.reshape(t_s, n_s, D_MODEL).transpose(1, 0, 2)

    def unpack(new, nb):
        stack = lambda i, f: jnp.stack([f(new[l][i]) for l in range(DEPTH)], axis=0)
        return (stack(0, lambda s: _conv_state_out(s, nb)), stack(1, lambda s: s),
                stack(2, lambda s: _conv_state_out(s, nb)),
                stack(3, lambda s: s.reshape(nb, G_C, P_C)),
                stack(4, lambda s: s.reshape(nb, G_C, P_C)))

    return (y_prompt, y_sample) + unpack(p_new, n_p) + unpack(s_new, n_s)
```

```python
import functools
import math

import jax
import jax.numpy as jnp
from jax import lax
from jax.experimental import pallas as pl
from jax.experimental.pallas import tpu as pltpu

D_MODEL = 1024
DEPTH = 2
W_A = D_MODEL
N_BLOCKS_A = 8
BLK_A = W_A // N_BLOCKS_A
CONV_A = 4
LRU_C = 8.0
W_B = D_MODEL // 2
CONV_B = 31
W_C = D_MODEL // 2
GS_C = 16
G_C = W_C // GS_C
P_C = 64
S_C = G_C * P_C
N_BRANCH = 3
IN_W = W_A + 2 * W_B + W_C + N_BRANCH * D_MODEL
D_FF = int(math.ceil(8 * D_MODEL / 3 / 256) * 256)
EPS = 1e-6

V7X_SUBLANES = 8
V7X_LANES = 128
V7X_MXU_DIM = 256
V7X_VMEM_BYTES = 64 * 1024 * 1024

OFF_ZB = W_A
OFF_UC = W_A + 2 * W_B
OFF_GATE = W_A + 2 * W_B + W_C
S5_LANE_GROUP = 4 * V7X_LANES
S5_CHUNKS = W_C // V7X_LANES
S5_CHUNK_STATE = S_C // S5_CHUNKS
PROMPT_TT = 32
PROMPT_FFN_ROWS = 512
LOG2_E = 1.4426950408889634
CONV_TILE_BLOCK = 8

BF16 = jnp.bfloat16
F32 = jnp.float32


def _sigmoid(x):
    return 1.0 / (1.0 + jnp.exp2(x * (-LOG2_E)))


def _silu(x):
    return x * _sigmoid(x)


def _gelu_tanh(x):
    c = math.sqrt(2.0 / math.pi)
    return 0.5 * x * (1.0 + jnp.tanh(c * (x + 0.044715 * (x * x * x))))


def _softplus(x):
    return jnp.maximum(x, 0.0) + jnp.log1p(jnp.exp(-jnp.abs(x)))


def _rms_norm(x, g):
    return x * lax.rsqrt(jnp.mean(x * x, axis=-1, keepdims=True) + EPS) * g


def _dot(a, b):
    return jnp.dot(a, b, preferred_element_type=F32)


def _dwconv(src, w_ref, bias_ref, store, taps, qnb, n_out, lane_chunks):
    for lc in lane_chunks:
        lanes = slice(lc * V7X_LANES, (lc + 1) * V7X_LANES)
        bias = jnp.broadcast_to(bias_ref[:, lanes], (V7X_SUBLANES, V7X_LANES))
        w = [w_ref[k, :, lanes] for k in range(taps)]
        for q0 in range(0, n_out, CONV_TILE_BLOCK):
            outs = range(q0, min(q0 + CONV_TILE_BLOCK, n_out))
            accs = {q: bias for q in outs}
            for i in sorted({q + k * qnb for q in outs for k in range(taps)}):
                x_i = src(i, lanes)
                for q in outs:
                    k, rem = divmod(i - q, qnb)
                    if rem == 0 and 0 <= k < taps:
                        accs[q] = accs[q] + w[k] * x_i
            for q in outs:
                store(q, lanes, accs[q])


def _conv_src(hist_ref, buf_ref, hist_tiles, single_step, base=0):
    if single_step:
        def src(i, lanes):
            if i < hist_tiles:
                return hist_ref[i, :, lanes]
            return buf_ref[base + (i - hist_tiles), :, lanes]
    else:
        def src(i, lanes):
            return buf_ref[base + i, :, lanes]
    return src


def _mixer_kernel(
        tile0_ref, x_ref, st_ca_ref, st_h_ref, st_cb_ref, st_sr_ref, st_si_ref,
        g_mix_ref, w_in_ref, wca_ref, bca_ref, wrgig_ref, b_rg_ref, b_ig_ref, lam_a_ref,
        wdb_ref, bdb_ref, ln_g_ref, ln_b_ref,
        ar_ref, ai_ref, bmat_ref, cre_ref, cim_ref, d_ssm_ref, w_glu_ref, b_glu_ref,
        b_gate_ref, w_pa_ref, w_pb_ref, w_pc_ref, w_out_ref,
        xo_ref, o_ca_ref, o_h_ref, o_cb_ref, o_sr_ref, o_si_ref,
        h_bf, ua_buf, ca_buf, gb_buf, a_buf, bx_buf, uc_buf, xr_buf, xi_buf, gates_buf,
        merged_buf, ya_bf, yb_bf, ycg_buf, ycg_bf, yc_bf, merged_bf, h_st, sr_st, si_st,
        *, nb, tt, n_steps):
    rows = nb * tt
    qnb = nb // V7X_SUBLANES
    qrows = rows // V7X_SUBLANES
    step = pl.program_id(0)
    single_step = tt == 1
    hist_a = 0 if single_step else (CONV_A - 1) * qnb
    hist_b = 0 if single_step else (CONV_B - 1) * qnb
    nblk = V7X_MXU_DIM
    blk = lambda q: slice(q * nblk, (q + 1) * nblk)
    to_tiles = lambda v: v.reshape(qrows, V7X_SUBLANES, v.shape[-1])

    @pl.when(step == 0)
    def _():
        if not single_step:
            ua_buf[pl.ds(0, hist_a)] = st_ca_ref[...]
            gb_buf[pl.ds(0, hist_b)] = st_cb_ref[...]
        h_st[...] = st_h_ref[...]
        sr_st[...] = st_sr_ref[...]
        si_st[...] = st_si_ref[...]

    def in_proj(col0):
        return _dot(h_bf[...], w_in_ref[:, col0:col0 + nblk])

    def v_rms():
        h_bf[...] = _rms_norm(x_ref[...], g_mix_ref[...]).astype(BF16)

    def m_z(p):
        val = in_proj(OFF_ZB + p * nblk)
        gate = in_proj(OFF_ZB + W_B + p * nblk)
        gb_buf[pl.ds(hist_b, qrows), :, blk(p)] = to_tiles(val * _sigmoid(gate))

    cb_out = hist_b + qrows

    def store_cb(q, lanes, val):
        gb_buf[tile0_ref[1] + (cb_out + q), :, lanes] = val

    def v_conv_b(lc):
        _dwconv(_conv_src(st_cb_ref, gb_buf, (CONV_B - 1) * qnb, single_step, tile0_ref[0]),
                wdb_ref, bdb_ref, store_cb, CONV_B, qnb, qrows, (lc,))

    def v_ln():
        c_b = gb_buf[pl.ds(cb_out, qrows)].reshape(rows, W_B)
        mu = jnp.mean(c_b, axis=-1, keepdims=True)
        xc = c_b - mu
        ln = xc * lax.rsqrt(jnp.mean(xc * xc, axis=-1, keepdims=True) + EPS)
        yb_bf[...] = _silu(ln * ln_g_ref[...] + ln_b_ref[...]).astype(BF16)

    def m_pb(q):
        merged_buf[:, blk(q)] = (gates_buf[:, D_MODEL + q * nblk:D_MODEL + (q + 1) * nblk]
                                 * _dot(yb_bf[...], w_pb_ref[:, blk(q)]))

    def m_ua(q):
        ua_buf[pl.ds(hist_a, qrows), :, blk(q)] = to_tiles(in_proj(q * nblk))

    def store_ca(q, lanes, val):
        ca_buf[q, :, lanes] = val

    def v_conv_a(lcs):
        _dwconv(_conv_src(st_ca_ref, ua_buf, (CONV_A - 1) * qnb, single_step), wca_ref, bca_ref,
                store_ca, CONV_A, qnb, qrows, lcs)

    def m_rg(i):
        sl = slice(i * BLK_A, (i + 1) * BLK_A)
        c_blk = ca_buf[:, :, sl].reshape(rows, BLK_A)
        pre = _dot(c_blk.astype(BF16), wrgig_ref[i])
        r = _sigmoid(pre[:, :BLK_A] + b_rg_ref[:, sl])
        ig = _sigmoid(pre[:, BLK_A:] + b_ig_ref[:, sl])
        log_a = r * (-LRU_C * _softplus(-lam_a_ref[:, sl]))
        a = jnp.exp(log_a)
        a_buf[:, sl] = a
        bx_buf[:, sl] = jnp.sqrt(-jnp.tanh(log_a) * (a * a + 1.0)) * ig * c_blk

    def v_lru():
        h_last = h_st[...]
        for t in range(tt):
            rs = slice(t * nb, (t + 1) * nb)
            h_last = a_buf[rs, :] * h_last + bx_buf[rs, :]
            bx_buf[rs, :] = h_last
        h_st[...] = h_last
        ya_bf[...] = bx_buf[...].astype(BF16)

    def m_pa(q):
        merged_buf[:, blk(q)] = (gates_buf[:, blk(q)] * _dot(ya_bf[...], w_pa_ref[:, blk(q)])
                                 + merged_buf[:, blk(q)])

    def m_uc(q):
        uc_buf[:, blk(q)] = in_proj(OFF_UC + q * nblk)

    def m_b(j):
        xcat = _dot(uc_buf[:, j * V7X_LANES:(j + 1) * V7X_LANES].astype(BF16), bmat_ref[j])
        sl = slice(j * S5_CHUNK_STATE, (j + 1) * S5_CHUNK_STATE)
        xr_buf[:, sl] = xcat[:, :S5_CHUNK_STATE]
        xi_buf[:, sl] = xcat[:, S5_CHUNK_STATE:]

    def v_s5(g4):
        sl = slice(g4 * S5_LANE_GROUP, (g4 + 1) * S5_LANE_GROUP)
        ar_b = jnp.broadcast_to(ar_ref[:, sl], (nb, S5_LANE_GROUP))
        ai_b = jnp.broadcast_to(ai_ref[:, sl], (nb, S5_LANE_GROUP))
        nsr, nsi = sr_st[:, sl], si_st[:, sl]
        for t in range(tt):
            rs = slice(t * nb, (t + 1) * nb)
            nsr, nsi = (ar_b * nsr - ai_b * nsi + xr_buf[rs, sl],
                        ar_b * nsi + ai_b * nsr + xi_buf[rs, sl])
            xr_buf[rs, sl] = nsr
            xi_buf[rs, sl] = nsi
        sr_st[:, sl] = nsr
        si_st[:, sl] = nsi

    def m_c(j):
        sl = slice(j * S5_CHUNK_STATE, (j + 1) * S5_CHUNK_STATE)
        ch = slice(j * V7X_LANES, (j + 1) * V7X_LANES)
        y = (_dot(xr_buf[:, sl].astype(BF16), cre_ref[j])
             - _dot(xi_buf[:, sl].astype(BF16), cim_ref[j])) + d_ssm_ref[:, ch] * uc_buf[:, ch]
        y = _gelu_tanh(y)
        ycg_buf[:, ch] = y
        ycg_bf[:, ch] = y.astype(BF16)

    def m_glu(p):
        gate = _sigmoid(_dot(ycg_bf[...], w_glu_ref[:, blk(p)]) + b_glu_ref[:, blk(p)])
        yc_bf[:, blk(p)] = (ycg_buf[:, blk(p)] * gate).astype(BF16)

    def m_pc(q):
        merged = merged_buf[:, blk(q)] + (
            gates_buf[:, 2 * D_MODEL + q * nblk:2 * D_MODEL + (q + 1) * nblk]
            * _dot(yc_bf[...], w_pc_ref[:, blk(q)]))
        merged_bf[:, blk(q)] = merged.astype(BF16)

    def m_g(n):
        gates_buf[:, blk(n)] = _sigmoid(in_proj(OFF_GATE + n * nblk) + b_gate_ref[:, blk(n)])

    def m_out(q):
        xo_ref[:, blk(q)] = x_ref[:, blk(q)] + _dot(merged_bf[...], w_out_ref[:, blk(q)])

    n_lc_a = W_A // V7X_LANES
    schedule = [
        (v_rms,), (m_z, 0), (m_z, 1),
        (m_uc, 0), (m_uc, 1), (v_conv_b, 0),
        (m_b, 0), (m_b, 1), (v_conv_b, 1),
        (m_b, 2), (m_b, 3), (v_conv_b, 2),
        (m_ua, 0), (m_ua, 1), (v_s5, 0), (v_s5, 1),
        (m_ua, 2), (m_ua, 3), (v_s5, 2), (v_s5, 3),
        (m_g, 0), (m_g, 1), (v_conv_b, 3),
        (m_g, 2), (m_g, 3), (v_conv_a, tuple(range(n_lc_a))),
        (m_c, 0), (m_c, 1), (m_c, 2), (m_c, 3), (v_ln,),
    ]
    for i in range(N_BLOCKS_A):
        schedule += [(m_rg, i), (m_g, 4 + i)]
    schedule += [
        (m_glu, 0), (m_glu, 1),
        (v_lru,), (m_pb, 0), (m_pb, 1), (m_pb, 2), (m_pb, 3),
        (m_pa, 0), (m_pa, 1), (m_pa, 2), (m_pa, 3),
        (m_pc, 0), (m_pc, 1), (m_pc, 2), (m_pc, 3),
        (m_out, 0), (m_out, 1), (m_out, 2), (m_out, 3),
    ]
    for task, *args in schedule:
        task(*args)

    if single_step:
        o_ca_ref[pl.ds(0, (CONV_A - 2) * qnb)] = st_ca_ref[pl.ds(qnb, (CONV_A - 2) * qnb)]
        o_ca_ref[pl.ds((CONV_A - 2) * qnb, qnb)] = ua_buf[...]
        o_cb_ref[pl.ds(0, (CONV_B - 2) * qnb)] = st_cb_ref[pl.ds(qnb, (CONV_B - 2) * qnb)]
        o_cb_ref[pl.ds((CONV_B - 2) * qnb, qnb)] = gb_buf[pl.ds(0, qnb)]
    else:
        new_ca = ua_buf[pl.ds(tt * qnb, hist_a)]
        new_cb = gb_buf[pl.ds(tt * qnb, hist_b)]
        ua_buf[pl.ds(0, hist_a)] = new_ca
        gb_buf[pl.ds(0, hist_b)] = new_cb

    @pl.when(step == n_steps - 1)
    def _():
        if not single_step:
            o_ca_ref[...] = ua_buf[pl.ds(0, hist_a)]
            o_cb_ref[...] = gb_buf[pl.ds(0, hist_b)]
        o_h_ref[...] = h_st[...]
        o_sr_ref[...] = sr_st[...]
        o_si_ref[...] = si_st[...]


def _ffn_kernel(x_ref, g_ffn_ref, w_fi_ref, w_fo_ref, g_fin_ref, o_ref, act_buf, *, final_norm):
    x = x_ref[...]
    h2 = _rms_norm(x, g_ffn_ref[...]).astype(BF16)
    for c in range(D_FF // V7X_MXU_DIM):
        lo = c * V7X_MXU_DIM
        gate = _dot(h2, w_fi_ref[:, lo:lo + V7X_MXU_DIM])
        up = _dot(h2, w_fi_ref[:, D_FF + lo:D_FF + lo + V7X_MXU_DIM])
        act_buf[:, lo:lo + V7X_MXU_DIM] = (_silu(gate) * up).astype(BF16)
    y = x + _dot(act_buf[...], w_fo_ref[...])
    if final_norm:
        y = _rms_norm(y, g_fin_ref[...])
    o_ref[...] = y


def _s5_prep_kernel(lam_re_ref, lam_im_ref, log_dt_ref, b_re_ref, b_im_ref,
                    ar_ref, ai_ref, bbr_ref, bbi_ref):
    dt = jnp.exp(log_dt_ref[...])
    lr, li = lam_re_ref[...], lam_im_ref[...]
    mag = jnp.exp(lr * dt)
    ar, ai = mag * jnp.cos(li * dt), mag * jnp.sin(li * dt)
    den = lr * lr + li * li
    qr = ((ar - 1.0) * lr + ai * li) / den
    qi = (ai * lr - (ar - 1.0) * li) / den
    br, bi = b_re_ref[...], b_im_ref[...]
    ar_ref[...] = ar
    ai_ref[...] = ai
    bbr_ref[...] = qr * br - qi * bi
    bbi_ref[...] = qr * bi + qi * br


def _const_spec(shape):
    return pl.BlockSpec(shape, lambda i, _n=len(shape): (0,) * _n, pipeline_mode=pl.Buffered(1))


def _layer_spec(stacked_shape, layer):
    n = len(stacked_shape) - 1
    return pl.BlockSpec((None,) + tuple(stacked_shape[1:]), lambda i: (layer,) + (0,) * n,
                        pipeline_mode=pl.Buffered(1))


def _nbytes(shape, dtype):
    return math.prod(shape) * jnp.dtype(dtype).itemsize


def _vmem_limit(estimate):
    return min(estimate, V7X_VMEM_BYTES - 6 * 1024 * 1024)


def _mixer_call(x, states, weights, layer, *, nb, tt):
    n_rows = x.shape[0]
    rows = nb * tt
    n_steps = n_rows // rows
    assert n_steps * rows == n_rows and nb % V7X_SUBLANES == 0
    assert (tt == 1 and n_steps == 1) or tt >= CONV_B - 1
    qnb = nb // V7X_SUBLANES
    hist_steps = 0 if tt == 1 else 1
    x_spec = pl.BlockSpec((rows, D_MODEL), lambda i: (i, 0))
    st_shapes = [((CONV_A - 1) * qnb, V7X_SUBLANES, W_A), (nb, W_A),
                 ((CONV_B - 1) * qnb, V7X_SUBLANES, W_B), (nb, S_C), (nb, S_C)]
    scratch = [
        ((rows, D_MODEL), BF16),
        (((hist_steps * (CONV_A - 1) + tt) * qnb, V7X_SUBLANES, W_A), F32),
        ((tt * qnb, V7X_SUBLANES, W_A), F32),
        (((hist_steps * (CONV_B - 1) + 2 * tt) * qnb, V7X_SUBLANES, W_B), F32),
        ((rows, W_A), F32), ((rows, W_A), F32),
        ((rows, W_C), F32), ((rows, S_C), F32), ((rows, S_C), F32),
        ((rows, N_BRANCH * D_MODEL), F32), ((rows, D_MODEL), F32),
        ((rows, W_A), BF16), ((rows, W_B), BF16),
        ((rows, W_C), F32), ((rows, W_C), BF16), ((rows, W_C), BF16),
        ((rows, D_MODEL), BF16),
        ((nb, W_A), F32), ((nb, S_C), F32), ((nb, S_C), F32)]
    w_bytes = sum(_nbytes(w.shape[1:], w.dtype) for w in weights)
    est = (w_bytes + 4 * _nbytes((rows, D_MODEL), F32)
           + 3 * sum(_nbytes(s, F32) for s in st_shapes)
           + sum(_nbytes(s, d) for s, d in scratch)
           + 8 * _nbytes((rows, D_MODEL), F32))
    out = pl.pallas_call(
        functools.partial(_mixer_kernel, nb=nb, tt=tt, n_steps=n_steps),
        grid=(n_steps,),
        in_specs=[pl.BlockSpec(memory_space=pltpu.SMEM), x_spec]
                 + [_const_spec(s) for s in st_shapes]
                 + [_layer_spec(w.shape, layer) for w in weights],
        out_specs=[x_spec] + [pl.BlockSpec(s, lambda i, _n=len(s): (0,) * _n) for s in st_shapes],
        out_shape=[jax.ShapeDtypeStruct((n_rows, D_MODEL), F32)]
                  + [jax.ShapeDtypeStruct(s, F32) for s in st_shapes],
        scratch_shapes=[pltpu.VMEM(s, d) for s, d in scratch],
        compiler_params=pltpu.CompilerParams(
            dimension_semantics=("arbitrary",), vmem_limit_bytes=_vmem_limit(est)),
        name="mixer_nb%d" % nb,
    )(jnp.zeros((2,), jnp.int32), x, *states, *weights)
    return out[0], out[1:]


def _ffn_call(x, weights, g_fin, layer, *, rows, final_norm):
    n_rows = x.shape[0]
    n_steps = n_rows // rows
    assert n_steps * rows == n_rows
    x_spec = pl.BlockSpec((rows, D_MODEL), lambda i: (i, 0))
    est = (sum(_nbytes(w.shape[1:], w.dtype) for w in weights) + 4 * _nbytes((rows, D_MODEL), F32)
           + _nbytes((rows, D_FF), BF16) + 8 * _nbytes((rows, D_MODEL), F32))
    return pl.pallas_call(
        functools.partial(_ffn_kernel, final_norm=final_norm),
        grid=(n_steps,),
        in_specs=[x_spec] + [_layer_spec(w.shape, layer) for w in weights]
                 + [_const_spec(g_fin.shape)],
        out_specs=x_spec,
        out_shape=jax.ShapeDtypeStruct((n_rows, D_MODEL), F32),
        scratch_shapes=[pltpu.VMEM((rows, D_FF), BF16)],
        compiler_params=pltpu.CompilerParams(
            dimension_semantics=("arbitrary",), vmem_limit_bytes=_vmem_limit(est)),
        name="ffn_r%d" % rows,
    )(x, *weights, g_fin)


def _s5_prep(lam_re, lam_im, log_dt, b_ssm_re, b_ssm_im):
    n_l = lam_re.shape[0]
    rep = lambda a: jnp.repeat(a.reshape(n_l * G_C, -1), GS_C, axis=0)
    lam_re_r, lam_im_r = rep(lam_re), rep(lam_im)
    log_dt_r = jnp.broadcast_to(rep(log_dt), (n_l * G_C * GS_C, P_C))
    to_rows = lambda b: b.transpose(0, 1, 3, 2).reshape(n_l * G_C * GS_C, P_C)
    shp = jax.ShapeDtypeStruct((n_l * G_C * GS_C, P_C), F32)
    ar, ai, bbr, bbi = pl.pallas_call(_s5_prep_kernel, out_shape=[shp] * 4, name="s5_prep")(
        lam_re_r, lam_im_r, log_dt_r, to_rows(b_ssm_re), to_rows(b_ssm_im))
    first = lambda a: a.reshape(n_l, G_C, GS_C, P_C)[:, :, 0, :].reshape(n_l, 1, S_C)
    return first(ar), first(ai), bbr.reshape(n_l, G_C, GS_C, P_C), bbi.reshape(n_l, G_C, GS_C, P_C)


def _stacked_weights(p, ar, ai, bbr, bbi):
    n_l = ar.shape[0]
    gpc = V7X_LANES // GS_C
    eye = jnp.eye(gpc, dtype=F32)

    def in_blockdiag(bb):
        b5 = bb.reshape(n_l, S5_CHUNKS, gpc, GS_C, P_C)
        return jnp.einsum('ljgcp,gh->ljgchp', b5, eye).reshape(
            n_l, S5_CHUNKS, V7X_LANES, gpc * P_C)

    def out_blockdiag(cc):
        c5 = cc.reshape(n_l, S5_CHUNKS, gpc, GS_C, P_C)
        return jnp.einsum('ljgcp,gh->ljgphc', c5, eye).reshape(
            n_l, S5_CHUNKS, gpc * P_C, V7X_LANES)

    bmat = jnp.concatenate([in_blockdiag(bbr), in_blockdiag(bbi)], axis=-1).astype(BF16)
    row = lambda a: a.reshape(n_l, 1, -1)
    bc8 = lambda a: jnp.broadcast_to(a[:, :, None, :], a.shape[:2] + (V7X_SUBLANES, a.shape[2]))
    bf = lambda a: a.astype(BF16)
    mixer = (
        row(p['g_mix']), bf(p['w_in']), bc8(p['w_conv_a']), row(p['b_conv_a']),
        bf(jnp.concatenate([p['w_rg'], p['w_ig']], axis=-1)),
        row(p['b_rg']), row(p['b_ig']), row(p['lam_a']),
        bc8(p['w_dw_b']), row(p['b_dw_b']), row(p['ln_g_b']), row(p['ln_b_b']),
        ar, ai, bmat, bf(out_blockdiag(p['c_ssm_re'])), bf(out_blockdiag(p['c_ssm_im'])),
        row(p['d_ssm']), bf(p['w_glu_c']), row(p['b_glu_c']),
        row(p['b_gate']), bf(p['w_pa']), bf(p['w_pb']), bf(p['w_pc']), bf(p['w_out']))
    ffn = (row(p['g_ffn']), bf(p['w_ffn_in']), bf(p['w_ffn_out']))
    return mixer, ffn


def _trunk(x, states, mixer_w, ffn_w, g_final, *, nb, tt, ffn_rows):
    new_states = []
    for l in range(DEPTH):
        x, st = _mixer_call(x, states[l], mixer_w, l, nb=nb, tt=tt)
        x = _ffn_call(x, ffn_w, g_final, l, rows=ffn_rows, final_norm=(l == DEPTH - 1))
        new_states.append(st)
    return x, new_states


def _conv_state_in(s, nb):
    k1, c = s.shape[1], s.shape[2]
    return s.transpose(1, 0, 2).reshape(k1 * nb // V7X_SUBLANES, V7X_SUBLANES, c)


def _conv_state_out(s, nb):
    c = s.shape[-1]
    return s.reshape(-1, nb, c).transpose(1, 0, 2)


def kernel(x_prompt, x_sample, state_lru_conv, state_lru_h, state_cfm_conv, state_ssm_re, state_ssm_im, g_mix, w_in, w_conv_a, b_conv_a, w_rg, b_rg, w_ig, b_ig, lam_a, w_dw_b, b_dw_b, ln_g_b, ln_b_b, lam_re, lam_im, log_dt, b_ssm_re, b_ssm_im, c_ssm_re, c_ssm_im, d_ssm, w_glu_c, b_glu_c, b_gate, w_pa, w_pb, w_pc, w_out, g_ffn, w_ffn_in, w_ffn_out, g_final):
    p = dict(g_mix=g_mix, w_in=w_in, w_conv_a=w_conv_a, b_conv_a=b_conv_a, w_rg=w_rg, b_rg=b_rg,
             w_ig=w_ig, b_ig=b_ig, lam_a=lam_a, w_dw_b=w_dw_b, b_dw_b=b_dw_b, ln_g_b=ln_g_b,
             ln_b_b=ln_b_b, c_ssm_re=c_ssm_re, c_ssm_im=c_ssm_im, d_ssm=d_ssm, w_glu_c=w_glu_c,
             b_glu_c=b_glu_c, b_gate=b_gate, w_pa=w_pa, w_pb=w_pb, w_pc=w_pc, w_out=w_out,
             g_ffn=g_ffn, w_ffn_in=w_ffn_in, w_ffn_out=w_ffn_out)
    ar, ai, bbr, bbi = _s5_prep(lam_re, lam_im, log_dt, b_ssm_re, b_ssm_im)
    mixer_w, ffn_w = _stacked_weights(p, ar, ai, bbr, bbi)
    g_fin = g_final.reshape(1, D_MODEL)

    n_p, t_p, _ = x_prompt.shape
    xp = x_prompt.transpose(1, 0, 2).reshape(t_p * n_p, D_MODEL)
    zero_states = [(jnp.zeros(((CONV_A - 1) * n_p // V7X_SUBLANES, V7X_SUBLANES, W_A), F32),
                    jnp.zeros((n_p, W_A), F32),
                    jnp.zeros(((CONV_B - 1) * n_p // V7X_SUBLANES, V7X_SUBLANES, W_B), F32),
                    jnp.zeros((n_p, S_C), F32), jnp.zeros((n_p, S_C), F32))] * DEPTH
    yp, p_new = _trunk(xp, zero_states, mixer_w, ffn_w, g_fin, nb=n_p, tt=PROMPT_TT,
                       ffn_rows=PROMPT_FFN_ROWS)
    y_prompt = yp.reshape(t_p, n_p, D_MODEL).transpose(1, 0, 2)

    n_s, t_s, _ = x_sample.shape
    xs = x_sample.transpose(1, 0, 2).reshape(t_s * n_s, D_MODEL)
    s_states = [(_conv_state_in(state_lru_conv[l], n_s), state_lru_h[l],
                 _conv_state_in(state_cfm_conv[l], n_s),
                 state_ssm_re[l].reshape(n_s, S_C), state_ssm_im[l].reshape(n_s, S_C))
                for l in range(DEPTH)]
    ys, s_new = _trunk(xs, s_states, mixer_w, ffn_w, g_fin, nb=n_s, tt=t_s, ffn_rows=n_s * t_s)
    y_sample = ys.reshape(t_s, n_s, D_MODEL).transpose(1, 0, 2)

    def unpack(new, nb):
        stack = lambda i, f: jnp.stack([f(new[l][i]) for l in range(DEPTH)], axis=0)
        return (stack(0, lambda s: _conv_state_out(s, nb)), stack(1, lambda s: s),
                stack(2, lambda s: _conv_state_out(s, nb)),
                stack(3, lambda s: s.reshape(nb, G_C, P_C)),
                stack(4, lambda s: s.reshape(nb, G_C, P_C)))

    return (y_prompt, y_sample) + unpack(p_new, n_p) + unpack(s_new, n_s)
```
